```python
import math
import jax, jax.numpy as jnp
from jax import lax
import numpy as np

D_MODEL = 1024
BATCH = 2
SEQ = 8192
DEPTH = 1
DEC_BATCH = 128
DEC_SEQ = 1
PAST_LEN = 2048
PAGE_SIZE = 128

CONV_CH = D_MODEL // 2
CONV_WIDTH = 31
N_HEADS = 4
HEAD_DIM = 64
K_DIM = 2 * HEAD_DIM
V_DIM = 2 * HEAD_DIM
ATTN_CH = N_HEADS * V_DIM
MIX_WIDTH = CONV_CH + ATTN_CH
IN_WIDTH = 2 * CONV_CH + 2 * N_HEADS * K_DIM + N_HEADS * V_DIM
ROT_DIM = HEAD_DIM // 4
ROPE_THETA = 500000.0
D_FF = -(-8 * D_MODEL // (3 * 256)) * 256
Q_BLOCK = 128
NORM_EPS = 1e-6
NEG_INF = -1e30

kernel_name = "hybrid_conformerconv_diffattn_decode"


def rmsnorm(x, g):
    xf = x.astype(jnp.float32)
    y = xf * lax.rsqrt(jnp.mean(xf * xf, axis=-1, keepdims=True) + NORM_EPS)
    return (y * g.astype(jnp.float32)).astype(x.dtype)


def layernorm(x, g, b):
    xf = x.astype(jnp.float32)
    mu = jnp.mean(xf, axis=-1, keepdims=True)
    xc = xf - mu
    y = xc * lax.rsqrt(jnp.mean(xc * xc, axis=-1, keepdims=True) + NORM_EPS)
    return (y * g.astype(jnp.float32) + b.astype(jnp.float32)).astype(x.dtype)


def lambda_init_for(layer_idx):
    return 0.8 - 0.6 * math.exp(-0.3 * layer_idx)


def partial_rope(x, pos):
    half = ROT_DIM // 2
    inv_freq = ROPE_THETA ** (-jnp.arange(0, ROT_DIM, 2, dtype=jnp.float32) / ROT_DIM)
    ang = pos.astype(jnp.float32)[:, None] * inv_freq[None, :]
    cos = jnp.cos(ang)[None, :, None, None, :].astype(x.dtype)
    sin = jnp.sin(ang)[None, :, None, None, :].astype(x.dtype)
    x1, x2, rest = x[..., :half], x[..., half:ROT_DIM], x[..., ROT_DIM:]
    return jnp.concatenate([x1 * cos - x2 * sin, x2 * cos + x1 * sin, rest], axis=-1)


def causal_dwconv(u_ext, w, b):
    y = lax.conv_general_dilated(u_ext, w[:, None, :], window_strides=(1,), padding='VALID',
                                 dimension_numbers=('NWC', 'WIO', 'NWC'),
                                 feature_group_count=CONV_CH)
    return y + b


def diff_attn_core(q, k, v, q_pos, k_pos, lam):
    s = jnp.einsum('bqhcd,bkhcd->bhcqk', q, k).astype(jnp.float32) * (HEAD_DIM ** -0.5)
    mask = k_pos[None, :] <= q_pos[:, None]
    s = jnp.where(mask, s, NEG_INF)
    p = jax.nn.softmax(s, axis=-1)
    a = p[:, :, 0] - lam * p[:, :, 1]
    return jnp.einsum('bhqk,bkhd->bqhd', a.astype(v.dtype), v)


def hybrid_layer(x, c, conv_past, k_past, v_past, q_pos, k_pos, lam_init, blockwise,
                 g_mix, g_ffn, w_ada, b_ada, w_in, dw_w, dw_b, cln_g, cln_b,
                 lq1, lk1, lq2, lk2, subln_g, w_out, w_ffn_in, w_ffn_out):
    B, T, _ = x.shape
    mods = jnp.split(jax.nn.silu(c) @ w_ada + b_ada, 6, axis=-1)
    sh_m, sc_m, gt_m, sh_f, sc_f, gt_f = [m[:, None, :] for m in mods]

    h = rmsnorm(x, g_mix) * (1 + sc_m) + sh_m
    proj = h @ w_in
    conv_in, q, k, v = jnp.split(
        proj, [2 * CONV_CH, 2 * CONV_CH + N_HEADS * K_DIM, 2 * CONV_CH + 2 * N_HEADS * K_DIM], axis=-1)

    u = conv_in[..., :CONV_CH] * jax.nn.sigmoid(conv_in[..., CONV_CH:])
    u_ext = jnp.concatenate([conv_past, u], axis=1)
    conv_out = jax.nn.silu(layernorm(causal_dwconv(u_ext, dw_w, dw_b), cln_g, cln_b))
    new_conv = u_ext[:, -(CONV_WIDTH - 1):]

    q = partial_rope(q.reshape(B, T, N_HEADS, 2, HEAD_DIM), q_pos)
    k = partial_rope(k.reshape(B, T, N_HEADS, 2, HEAD_DIM), q_pos)
    v = v.reshape(B, T, N_HEADS, V_DIM)
    k_rows = k.reshape(B, T, N_HEADS, K_DIM)
    lam = (jnp.exp(jnp.sum(lq1.astype(jnp.float32) * lk1.astype(jnp.float32)))
           - jnp.exp(jnp.sum(lq2.astype(jnp.float32) * lk2.astype(jnp.float32))) + lam_init)
    if k_past is None:
        k_all, v_all = k, v
    else:
        k_all = jnp.concatenate([k_past.reshape(B, -1, N_HEADS, 2, HEAD_DIM), k], axis=1)
        v_all = jnp.concatenate([v_past, v], axis=1)
    if blockwise:
        nq = T // Q_BLOCK
        qb = q.reshape(B, nq, Q_BLOCK, N_HEADS, 2, HEAD_DIM).swapaxes(0, 1)
        qpb = q_pos.reshape(nq, Q_BLOCK)
        o = lax.map(lambda a: diff_attn_core(a[0], k_all, v_all, a[1], k_pos, lam), (qb, qpb))
        o = o.swapaxes(0, 1).reshape(B, T, N_HEADS, V_DIM)
    else:
        o = diff_attn_core(q, k_all, v_all, q_pos, k_pos, lam)
    o = (rmsnorm(o, subln_g) * (1 - lam_init)).reshape(B, T, ATTN_CH)

    mix = jnp.concatenate([conv_out, o], axis=-1) @ w_out
    x = x + gt_m * mix

    h2 = rmsnorm(x, g_ffn) * (1 + sc_f) + sh_f
    gg, uu = jnp.split(h2 @ w_ffn_in, 2, axis=-1)
    x = x + gt_f * ((jax.nn.silu(gg) * uu) @ w_ffn_out)
    return x, k_rows, v, new_conv


def setup_inputs(seed: int = 0) -> dict:
    key = jax.random.key(seed)
    ks = jax.random.split(key, 32)
    f32 = jnp.float32
    n_pages = PAST_LEN // PAGE_SIZE
    n_used = DEC_BATCH * n_pages
    n_pool = n_used + -(-n_used // 4)
    nrm = lambda k, shape, s: jax.random.normal(k, shape, f32) * s
    page_table = jax.random.permutation(ks[0], n_pool)[:n_used].reshape(DEC_BATCH, n_pages).astype(jnp.int32)
    return {
        "x_prompt": nrm(ks[1], (BATCH, SEQ, D_MODEL), 1.0),
        "x_sample": nrm(ks[2], (DEC_BATCH, DEC_SEQ, D_MODEL), 1.0),
        "cache_k": nrm(ks[3], (DEPTH, n_pool, PAGE_SIZE, N_HEADS, K_DIM), 1.0),
        "cache_v": nrm(ks[4], (DEPTH, n_pool, PAGE_SIZE, N_HEADS, V_DIM), 1.0),
        "state_conv": nrm(ks[5], (DEPTH, DEC_BATCH, CONV_WIDTH - 1, CONV_CH), 0.5),
        "page_table": page_table,
        "c_prompt": nrm(ks[6], (BATCH, D_MODEL), 1.0),
        "c_sample": nrm(ks[7], (DEC_BATCH, D_MODEL), 1.0),
        "norm_mix_g": 1.0 + nrm(ks[8], (DEPTH, D_MODEL), 0.05),
        "norm_ffn_g": 1.0 + nrm(ks[9], (DEPTH, D_MODEL), 0.05),
        "norm_final_g": 1.0 + nrm(ks[10], (D_MODEL,), 0.05),
        "w_ada": nrm(ks[11], (DEPTH, D_MODEL, 6 * D_MODEL), 0.5 * D_MODEL ** -0.5),
        "b_ada": nrm(ks[12], (DEPTH, 6 * D_MODEL), 0.02),
        "w_in": nrm(ks[13], (DEPTH, D_MODEL, IN_WIDTH), D_MODEL ** -0.5),
        "conv_dw_w": nrm(ks[14], (DEPTH, CONV_WIDTH, CONV_CH), CONV_WIDTH ** -0.5),
        "conv_dw_b": nrm(ks[15], (DEPTH, CONV_CH), 0.02),
        "conv_ln_g": 1.0 + nrm(ks[16], (DEPTH, CONV_CH), 0.05),
        "conv_ln_b": nrm(ks[17], (DEPTH, CONV_CH), 0.02),
        "lambda_q1": nrm(ks[18], (DEPTH, HEAD_DIM), 0.1),
        "lambda_k1": nrm(ks[19], (DEPTH, HEAD_DIM), 0.1),
        "lambda_q2": nrm(ks[20], (DEPTH, HEAD_DIM), 0.1),
        "lambda_k2": nrm(ks[21], (DEPTH, HEAD_DIM), 0.1),
        "subln_g": 1.0 + nrm(ks[22], (DEPTH, V_DIM), 0.05),
        "w_out": nrm(ks[23], (DEPTH, MIX_WIDTH, D_MODEL), MIX_WIDTH ** -0.5),
        "w_ffn_in": nrm(ks[24], (DEPTH, D_MODEL, 2 * D_FF), D_MODEL ** -0.5),
        "w_ffn_out": nrm(ks[25], (DEPTH, D_FF, D_MODEL), D_FF ** -0.5),
    }


def reference(x_prompt, x_sample, cache_k, cache_v, state_conv, page_table, c_prompt, c_sample,
              norm_mix_g, norm_ffn_g, norm_final_g, w_ada, b_ada, w_in, conv_dw_w, conv_dw_b,
              conv_ln_g, conv_ln_b, lambda_q1, lambda_k1, lambda_q2, lambda_k2, subln_g,
              w_out, w_ffn_in, w_ffn_out):
    B, S, _ = x_prompt.shape
    DB, T, _ = x_sample.shape
    n_pages = page_table.shape[1]
    past = n_pages * cache_k.shape[2]
    pos_p = jnp.arange(S, dtype=jnp.int32)
    pos_s = past + jnp.arange(T, dtype=jnp.int32)
    kpos_s = jnp.arange(past + T, dtype=jnp.int32)
    conv_zero = jnp.zeros((B, CONV_WIDTH - 1, CONV_CH), x_prompt.dtype)

    xp, xs = x_prompt, x_sample
    kp_l, vp_l, cp_l, ks_l, vs_l, cs_l = [], [], [], [], [], []
    for l in range(DEPTH):
        lam_init = lambda_init_for(l)
        w = (norm_mix_g[l], norm_ffn_g[l], w_ada[l], b_ada[l], w_in[l], conv_dw_w[l], conv_dw_b[l],
             conv_ln_g[l], conv_ln_b[l], lambda_q1[l], lambda_k1[l], lambda_q2[l], lambda_k2[l],
             subln_g[l], w_out[l], w_ffn_in[l], w_ffn_out[l])
        k_past = cache_k[l][page_table].reshape(DB, past, N_HEADS, K_DIM)
        v_past = cache_v[l][page_table].reshape(DB, past, N_HEADS, V_DIM)
        xp, kp, vp, cp = hybrid_layer(xp, c_prompt, conv_zero, None, None, pos_p, pos_p, lam_init, True, *w)
        xs, ksn, vsn, csn = hybrid_layer(xs, c_sample, state_conv[l], k_past, v_past, pos_s, kpos_s,
                                         lam_init, False, *w)
        kp_l.append(kp); vp_l.append(vp); cp_l.append(cp)
        ks_l.append(ksn); vs_l.append(vsn); cs_l.append(csn)

    y_prompt = rmsnorm(xp, norm_final_g)
    y_sample = rmsnorm(xs, norm_final_g)
    k_prompt = jnp.stack(kp_l)
    v_prompt = jnp.stack(vp_l)
    conv_prompt = jnp.stack(cp_l)
    k_sample = jnp.stack(ks_l)
    v_sample = jnp.stack(vs_l)
    conv_sample = jnp.stack(cs_l)
    return (y_prompt, y_sample, k_prompt, v_prompt, conv_prompt, k_sample, v_sample, conv_sample)
```

```python
import functools
import math

import jax
import jax.numpy as jnp
from jax import lax
from jax.experimental import pallas as pl
from jax.experimental.pallas import tpu as pltpu

F32 = jnp.float32
BF16 = jnp.bfloat16

D_MODEL = 1024
CONV_CH = 512
CONV_WIDTH = 31
N_HEADS = 4
HEAD_DIM = 64
HEAD_W = 2 * HEAD_DIM
ATTN_CH = N_HEADS * HEAD_W
ROT_DIM = HEAD_DIM // 4
ROPE_THETA = 500000.0
NORM_EPS = 1e-6
NEG_INF = -1e30
QK_SCALE = HEAD_DIM ** -0.5

VMEM_LIMIT = 56 * 1024 * 1024

HIST = 32
CONV_ROWS = 64


def _lambda_init(layer_idx):
    return 0.8 - 0.6 * math.exp(-0.3 * layer_idx)


def _dot(a, b):
    return jnp.dot(a, b, preferred_element_type=F32)


def _dot_nt(a, b):
    return lax.dot_general(a, b, (((1,), (1,)), ((), ())), preferred_element_type=F32)


def _rms(x):
    return x * lax.rsqrt(jnp.mean(x * x, axis=-1, keepdims=True) + NORM_EPS)


def _sigmoid(x):
    return 1.0 / (1.0 + jnp.exp(-x))


def _silu(x):
    return x * _sigmoid(x)


def _rope(x, c, sa, sb):
    return x * c + pltpu.roll(x, HEAD_W - ROT_DIM // 2, 1) * sa + pltpu.roll(x, ROT_DIM // 2, 1) * sb


def _lam_vec(l_ref, lam_init):
    l = l_ref[...]
    s1 = jnp.sum(l[0:1] * l[1:2], axis=-1, keepdims=True)
    s2 = jnp.sum(l[2:3] * l[3:4], axis=-1, keepdims=True)
    return jnp.exp(s1) - jnp.exp(s2) + lam_init


def _ada_kernel(c_ref, w_ref, b_ref, o_ref):
    c = c_ref[...]
    o_ref[...] = _dot(_silu(c).astype(BF16), w_ref[...].astype(BF16)) + b_ref[...]


def _ada(c_all, w_ada, b_ada):
    rows = c_all.shape[0]
    n = w_ada.shape[1]
    tn = 1024
    return pl.pallas_call(
        _ada_kernel,
        out_shape=jax.ShapeDtypeStruct((rows, n), F32),
        grid=(n // tn,),
        in_specs=[pl.BlockSpec((rows, D_MODEL), lambda j: (0, 0)),
                  pl.BlockSpec((D_MODEL, tn), lambda j: (0, j)),
                  pl.BlockSpec((1, tn), lambda j: (0, j))],
        out_specs=pl.BlockSpec((rows, tn), lambda j: (0, j)),
        compiler_params=pltpu.CompilerParams(dimension_semantics=("arbitrary",),
                                             vmem_limit_bytes=VMEM_LIMIT),
        name="ada",
    )(c_all, w_ada, b_ada)


def _inproj_p_kernel(x_ref, sh_ref, sc_ref, g_ref, w_ref, dww_ref, dwb_ref, lng_ref, lnb_ref,
                     rc_ref, ra_ref, rb_ref,
                     kf_ref, vf_ref, qb_ref, kb_ref, vb_ref, co_ref, cst_ref, hist_ref, *, tm):
    t = pl.program_id(1)

    @pl.when(t == 0)
    def _():
        hist_ref[0:HIST, :] = jnp.zeros((HIST, CONV_CH), F32)

    x = x_ref[...]
    h = (_rms(x) * g_ref[...]) * (1.0 + sc_ref[...]) + sh_ref[...]
    hb = h.astype(BF16)

    a = _dot(hb, w_ref[:, 0:CONV_CH])
    gate = _dot(hb, w_ref[:, CONV_CH:2 * CONV_CH])
    hist_ref[HIST:HIST + tm, :] = a * _sigmoid(gate)

    base = 2 * CONV_CH
    rc, ra, rb = rc_ref[...], ra_ref[...], rb_ref[...]
    for hd in range(N_HEADS):
        lo, hi = hd * HEAD_W, (hd + 1) * HEAD_W
        q = _rope(_dot(hb, w_ref[:, base + lo:base + hi]), rc, ra, rb)
        qb_ref[:, lo:hi] = (q * QK_SCALE).astype(BF16)
        k = _rope(_dot(hb, w_ref[:, base + ATTN_CH + lo:base + ATTN_CH + hi]), rc, ra, rb)
        kf_ref[:, lo:hi] = k
        kb_ref[:, lo:hi] = k.astype(BF16)
        v = _dot(hb, w_ref[:, base + 2 * ATTN_CH + lo:base + 2 * ATTN_CH + hi])
        vf_ref[:, lo:hi] = v
        vb_ref[:, lo:hi] = v.astype(BF16)

    off0 = HIST - (CONV_WIDTH - 1)

    for r in range(tm // CONV_ROWS):
        r0 = r * CONV_ROWS
        acc = jnp.zeros((CONV_ROWS, CONV_CH), F32) + dwb_ref[...]
        for j in range(CONV_WIDTH):
            acc = acc + dww_ref[j:j + 1, :] * hist_ref[r0 + off0 + j:r0 + off0 + j + CONV_ROWS, :]
        mu = jnp.mean(acc, axis=-1, keepdims=True)
        xc = acc - mu
        y = xc * lax.rsqrt(jnp.mean(xc * xc, axis=-1, keepdims=True) + NORM_EPS)
        y = y * lng_ref[...] + lnb_ref[...]
        co_ref[r0:r0 + CONV_ROWS, :] = _silu(y).astype(BF16)

    tail = hist_ref[tm:tm + HIST, :]
    hist_ref[0:HIST, :] = tail

    @pl.when(t == pl.num_programs(1) - 1)
    def _():
        cst_ref[...] = tail


def _inproj_p(x, sh, sc, g, w_in_b, dw_w, dw_b, ln_g, ln_b, rc, ra, rb, tm):
    B, S, _ = x.shape
    nt = S // tm
    row = lambda b, t: (b, t, 0)
    const2 = lambda b, t: (0, 0)
    kern = functools.partial(_inproj_p_kernel, tm=tm)
    return pl.pallas_call(
        kern,
        out_shape=(jax.ShapeDtypeStruct((B, S, ATTN_CH), F32),
                   jax.ShapeDtypeStruct((B, S, ATTN_CH), F32),
                   jax.ShapeDtypeStruct((B, S, ATTN_CH), BF16),
                   jax.ShapeDtypeStruct((B, S, ATTN_CH), BF16),
                   jax.ShapeDtypeStruct((B, S, ATTN_CH), BF16),
                   jax.ShapeDtypeStruct((B, S, CONV_CH), BF16),
                   jax.ShapeDtypeStruct((B, HIST, CONV_CH), F32)),
        grid=(B, nt),
        in_specs=[pl.BlockSpec((None, tm, D_MODEL), row),
                  pl.BlockSpec((None, 1, D_MODEL), lambda b, t: (b, 0, 0)),
                  pl.BlockSpec((None, 1, D_MODEL), lambda b, t: (b, 0, 1)),
                  pl.BlockSpec((1, D_MODEL), const2),
                  pl.BlockSpec(w_in_b.shape, const2, pipeline_mode=pl.Buffered(1)),
                  pl.BlockSpec(dw_w.shape, const2),
                  pl.BlockSpec((1, CONV_CH), const2),
                  pl.BlockSpec((1, CONV_CH), const2),
                  pl.BlockSpec((1, CONV_CH), const2),
                  pl.BlockSpec((tm, HEAD_W), lambda b, t: (t, 0)),
                  pl.BlockSpec((tm, HEAD_W), lambda b, t: (t, 0)),
                  pl.BlockSpec((tm, HEAD_W), lambda b, t: (t, 0))],
        out_specs=(pl.BlockSpec((None, tm, ATTN_CH), row),
                   pl.BlockSpec((None, tm, ATTN_CH), row),
                   pl.BlockSpec((None, tm, ATTN_CH), row),
                   pl.BlockSpec((None, tm, ATTN_CH), row),
                   pl.BlockSpec((None, tm, ATTN_CH), row),
                   pl.BlockSpec((None, tm, CONV_CH), row),
                   pl.BlockSpec((None, HIST, CONV_CH), lambda b, t: (b, 0, 0))),
        scratch_shapes=[pltpu.VMEM((HIST + tm + 8, CONV_CH), F32)],
        compiler_params=pltpu.CompilerParams(dimension_semantics=("arbitrary", "arbitrary"),
                                             vmem_limit_bytes=VMEM_LIMIT),
        name="inproj_p",
    )(x, sh, sc, g, w_in_b, dw_w, dw_b, ln_g, ln_b, rc, ra, rb)


def _attn_p_kernel(q_ref, k_ref, v_ref, lam_ref, sg_ref, o_ref, m_ref, l_ref, acc_ref, *, t, lam_init):
    qi = pl.program_id(2)
    q = q_ref[...]
    lane = lax.broadcasted_iota(jnp.int32, (t, HEAD_W), 1)
    zero = jnp.zeros_like(q)
    qbd = jnp.concatenate([jnp.where(lane < HEAD_DIM, q, zero),
                           jnp.where(lane >= HEAD_DIM, q, zero)], axis=0)

    m_ref[...] = jnp.full((2 * t, 1), NEG_INF, F32)
    l_ref[...] = jnp.zeros((2 * t, 1), F32)
    acc_ref[...] = jnp.zeros((2 * t, HEAD_W), F32)

    def step(j, masked):
        k0 = pl.multiple_of(j * t, t)
        s = _dot_nt(qbd, k_ref[pl.ds(k0, t), :])
        if masked:
            r = lax.broadcasted_iota(jnp.int32, (2 * t, t), 0)
            c = lax.broadcasted_iota(jnp.int32, (2 * t, t), 1)
            qpos = jnp.where(r >= t, r - t, r)
            s = jnp.where(c <= qpos, s, NEG_INF)
        m_old = m_ref[...]
        m_new = jnp.maximum(m_old, jnp.max(s, axis=-1, keepdims=True))
        alpha = jnp.exp(m_old - m_new)
        p = jnp.exp(s - m_new)
        l_ref[...] = alpha * l_ref[...] + jnp.sum(p, axis=-1, keepdims=True)
        pv = _dot(p.astype(BF16), v_ref[pl.ds(k0, t), :])
        acc_ref[...] = alpha * acc_ref[...] + pv
        m_ref[...] = m_new

    def body(j, carry):
        step(j, False)
        return carry

    lax.fori_loop(0, qi, body, 0)
    step(qi, True)

    lam = _lam_vec(lam_ref, lam_init)
    o = acc_ref[...] / l_ref[...]
    od = o[0:t] - lam * o[t:2 * t]
    o_ref[...] = ((_rms(od) * sg_ref[...]) * (1.0 - lam_init)).astype(o_ref.dtype)


def _attn_p(qb, kb, vb, lam4, subln_g, t, lam_init):
    B, S, _ = qb.shape
    kern = functools.partial(_attn_p_kernel, t=t, lam_init=lam_init)
    return pl.pallas_call(
        kern,
        out_shape=jax.ShapeDtypeStruct((B, S, ATTN_CH), BF16),
        grid=(B, N_HEADS, S // t),
        in_specs=[pl.BlockSpec((None, t, HEAD_W), lambda b, h, i: (b, i, h)),
                  pl.BlockSpec((None, S, HEAD_W), lambda b, h, i: (b, 0, h)),
                  pl.BlockSpec((None, S, HEAD_W), lambda b, h, i: (b, 0, h)),
                  pl.BlockSpec((4, HEAD_DIM), lambda b, h, i: (0, 0)),
                  pl.BlockSpec((1, HEAD_W), lambda b, h, i: (0, 0))],
        out_specs=pl.BlockSpec((None, t, HEAD_W), lambda b, h, i: (b, i, h)),
        scratch_shapes=[pltpu.VMEM((2 * t, 1), F32),
                        pltpu.VMEM((2 * t, 1), F32),
                        pltpu.VMEM((2 * t, HEAD_W), F32)],
        compiler_params=pltpu.CompilerParams(dimension_semantics=("arbitrary",) * 3,
                                             vmem_limit_bytes=VMEM_LIMIT),
        name="attn_p",
    )(qb, kb, vb, lam4, subln_g)


def _mlp_kernel(x_ref, co_ref, ao_ref, gtm_ref, shf_ref, scf_ref, gtf_ref, gffn_ref, gfin_ref,
                wo_ref, wi_ref, w2_ref, y_ref, *, d_ff, ck):
    mix = _dot(co_ref[...], wo_ref[0:CONV_CH, :]) + _dot(ao_ref[...], wo_ref[CONV_CH:2 * CONV_CH, :])
    x1 = x_ref[...] + gtm_ref[...] * mix
    h2 = ((_rms(x1) * gffn_ref[...]) * (1.0 + scf_ref[...]) + shf_ref[...]).astype(BF16)
    acc = jnp.zeros(x1.shape, F32)
    for c in range(d_ff // ck):
        gg = _dot(h2, wi_ref[:, c * ck:(c + 1) * ck])
        uu = _dot(h2, wi_ref[:, d_ff + c * ck:d_ff + (c + 1) * ck])
        acc = acc + _dot((_silu(gg) * uu).astype(BF16), w2_ref[c * ck:(c + 1) * ck, :])
    x2 = x1 + gtf_ref[...] * acc
    y_ref[...] = _rms(x2) * gfin_ref[...]


def _mlp(x, co, ao, mods, mod_specs, g_ffn, g_fin, wo_b, wi_b, w2_b, tm, name):
    R = x.shape[0]
    d_ff = w2_b.shape[0]
    ck = 256
    const = lambda i: (0, 0)
    kern = functools.partial(_mlp_kernel, d_ff=d_ff, ck=ck)
    return pl.pallas_call(
        kern,
        out_shape=jax.ShapeDtypeStruct((R, D_MODEL), F32),
        grid=(R // tm,),
        in_specs=[pl.BlockSpec((tm, D_MODEL), lambda i: (i, 0)),
                  pl.BlockSpec((tm, CONV_CH), lambda i: (i, 0)),
                  pl.BlockSpec((tm, ATTN_CH), lambda i: (i, 0)),
                  *mod_specs,
                  pl.BlockSpec((1, D_MODEL), const),
                  pl.BlockSpec((1, D_MODEL), const),
                  pl.BlockSpec(wo_b.shape, const, pipeline_mode=pl.Buffered(1)),
                  pl.BlockSpec(wi_b.shape, const, pipeline_mode=pl.Buffered(1)),
                  pl.BlockSpec(w2_b.shape, const, pipeline_mode=pl.Buffered(1))],
        out_specs=pl.BlockSpec((tm, D_MODEL), lambda i: (i, 0)),
        compiler_params=pltpu.CompilerParams(dimension_semantics=("arbitrary",),
                                             vmem_limit_bytes=VMEM_LIMIT),
        name=name,
    )(x, co, ao, mods, mods, mods, mods, g_ffn, g_fin, wo_b, wi_b, w2_b)


def _inproj_s_kernel(x_ref, sh_ref, sc_ref, g_ref, w_ref, st_ref, dww_ref, dwl_ref, dwb_ref, lng_ref, lnb_ref,
                     rc_ref, ra_ref, rb_ref,
                     u_ref, kf_ref, vf_ref, qb_ref, co_ref):
    x = x_ref[...]
    h = (_rms(x) * g_ref[...]) * (1.0 + sc_ref[...]) + sh_ref[...]
    hb = h.astype(BF16)
    a = _dot(hb, w_ref[:, 0:CONV_CH])
    gate = _dot(hb, w_ref[:, CONV_CH:2 * CONV_CH])
    u = a * _sigmoid(gate)
    u_ref[...] = u

    base = 2 * CONV_CH
    rc, ra, rb = rc_ref[...], ra_ref[...], rb_ref[...]
    for hd in range(N_HEADS):
        lo, hi = hd * HEAD_W, (hd + 1) * HEAD_W
        q = _rope(_dot(hb, w_ref[:, base + lo:base + hi]), rc, ra, rb)
        qb_ref[:, lo:hi] = (q * QK_SCALE).astype(BF16)
        kf_ref[:, lo:hi] = _rope(_dot(hb, w_ref[:, base + ATTN_CH + lo:base + ATTN_CH + hi]), rc, ra, rb)
        vf_ref[:, lo:hi] = _dot(hb, w_ref[:, base + 2 * ATTN_CH + lo:base + 2 * ATTN_CH + hi])

    past = jnp.sum(st_ref[...] * dww_ref[...][None, :, :], axis=1)
    acc = past + u * dwl_ref[...] + dwb_ref[...]
    mu = jnp.mean(acc, axis=-1, keepdims=True)
    xc = acc - mu
    y = xc * lax.rsqrt(jnp.mean(xc * xc, axis=-1, keepdims=True) + NORM_EPS)
    y = y * lng_ref[...] + lnb_ref[...]
    co_ref[...] = _silu(y).astype(BF16)


def _inproj_s(x, sh_sc, g, w_in_b, state, dw_past, dw_last, dw_b, ln_g, ln_b, rc, ra, rb):
    DB = x.shape[0]
    const = lambda i: (0, 0)
    return pl.pallas_call(
        _inproj_s_kernel,
        out_shape=(jax.ShapeDtypeStruct((DB, CONV_CH), F32),
                   jax.ShapeDtypeStruct((DB, ATTN_CH), F32),
                   jax.ShapeDtypeStruct((DB, ATTN_CH), F32),
                   jax.ShapeDtypeStruct((DB, ATTN_CH), BF16),
                   jax.ShapeDtypeStruct((DB, CONV_CH), BF16)),
        grid=(1,),
        in_specs=[pl.BlockSpec((DB, D_MODEL), const),
                  pl.BlockSpec((DB, D_MODEL), lambda i: (0, 0)),
                  pl.BlockSpec((DB, D_MODEL), lambda i: (0, 1)),
                  pl.BlockSpec((1, D_MODEL), const),
                  pl.BlockSpec(w_in_b.shape, const, pipeline_mode=pl.Buffered(1)),
                  pl.BlockSpec(state.shape, lambda i: (0, 0, 0), pipeline_mode=pl.Buffered(1)),
                  pl.BlockSpec(dw_past.shape, const),
                  pl.BlockSpec((1, CONV_CH), const),
                  pl.BlockSpec((1, CONV_CH), const),
                  pl.BlockSpec((1, CONV_CH), const),
                  pl.BlockSpec((1, CONV_CH), const),
                  pl.BlockSpec((1, HEAD_W), const),
                  pl.BlockSpec((1, HEAD_W), const),
                  pl.BlockSpec((1, HEAD_W), const)],
        out_specs=(pl.BlockSpec((DB, CONV_CH), const),
                   pl.BlockSpec((DB, ATTN_CH), const),
                   pl.BlockSpec((DB, ATTN_CH), const),
                   pl.BlockSpec((DB, ATTN_CH), const),
                   pl.BlockSpec((DB, CONV_CH), const)),
        compiler_params=pltpu.CompilerParams(dimension_semantics=("arbitrary",),
                                             vmem_limit_bytes=VMEM_LIMIT),
        name="inproj_s",
    )(x, sh_sc, sh_sc, g, w_in_b, state, dw_past, dw_last, dw_b, ln_g, ln_b, rc, ra, rb)


def _attn_s_kernel(pt_ref, q_ref, kn_ref, vn_ref, lam_ref, sg_ref, ck_ref, cv_ref, o_ref,
                   kbuf, vbuf, ksem, vsem, *, n_pages, page, lam_init):
    b = pl.program_id(0)
    nb = pl.num_programs(0)

    def copies(bb, slot):
        out = []
        for p in range(n_pages):
            pg = pt_ref[bb, p]
            out.append(pltpu.make_async_copy(ck_ref.at[pg], kbuf.at[slot, pl.ds(p * page, page)], ksem.at[slot]))
            out.append(pltpu.make_async_copy(cv_ref.at[pg], vbuf.at[slot, pl.ds(p * page, page)], vsem.at[slot]))
        return out

    slot = lax.rem(b, 2)

    @pl.when(b == 0)
    def _():
        for cp in copies(b, 0):
            cp.start()

    @pl.when(b + 1 < nb)
    def _():
        for cp in copies(b + 1, 1 - slot):
            cp.start()

    for cp in copies(b, slot):
        cp.wait()

    q = q_ref[...]
    kn = kn_ref[...]
    vn = vn_ref[...]
    rows = 2 * N_HEADS
    mrows = 16
    r = lax.broadcasted_iota(jnp.int32, (mrows, ATTN_CH), 0)
    l = lax.broadcasted_iota(jnp.int32, (mrows, ATTN_CH), 1)
    sel = ((l // HEAD_DIM) == (2 * (r % N_HEADS) + r // N_HEADS)) & (r < rows)
    qm = jnp.where(sel, jnp.broadcast_to(q.astype(F32), (mrows, ATTN_CH)), 0.0)

    kb = kbuf[slot].astype(BF16)
    s = _dot_nt(qm.astype(BF16), kb)[0:rows]
    s_new = jnp.sum(qm[0:rows] * kn, axis=-1, keepdims=True)
    m = jnp.maximum(jnp.max(s, axis=-1, keepdims=True), s_new)
    p = jnp.exp(s - m)
    p_new = jnp.exp(s_new - m)
    inv = 1.0 / (jnp.sum(p, axis=-1, keepdims=True) + p_new)
    lam = _lam_vec(lam_ref, lam_init)
    pn = p * inv
    pn_new = p_new * inv
    a = pn[0:N_HEADS] - lam * pn[N_HEADS:rows]
    a_new = pn_new[0:N_HEADS] - lam * pn_new[N_HEADS:rows]
    a16 = jnp.concatenate([a, jnp.zeros((mrows - N_HEADS, a.shape[1]), F32)], axis=0).astype(BF16)
    of = _dot(a16, vbuf[slot].astype(BF16))
    outs = []
    for hd in range(N_HEADS):
        lo, hi = hd * HEAD_W, (hd + 1) * HEAD_W
        oh = of[hd:hd + 1, lo:hi] + a_new[hd:hd + 1] * vn[:, lo:hi]
        outs.append((_rms(oh) * sg_ref[...]) * (1.0 - lam_init))
    o_ref[...] = jnp.concatenate(outs, axis=-1).astype(o_ref.dtype)


def _attn_s(page_table, qb, kn, vn, lam4, subln_g, cache_k, cache_v, lam_init):
    DB, n_pages = page_table.shape
    page = cache_k.shape[1]
    past = n_pages * page
    kern = functools.partial(_attn_s_kernel, n_pages=n_pages, page=page, lam_init=lam_init)
    rowspec = lambda dt: pl.BlockSpec((None, 1, ATTN_CH), lambda b, pt: (b, 0, 0))
    return pl.pallas_call(
        kern,
        out_shape=jax.ShapeDtypeStruct((DB, 1, ATTN_CH), BF16),
        grid_spec=pltpu.PrefetchScalarGridSpec(
            num_scalar_prefetch=1,
            grid=(DB,),
            in_specs=[rowspec(BF16), rowspec(F32), rowspec(F32),
                      pl.BlockSpec((4, HEAD_DIM), lambda b, pt: (0, 0)),
                      pl.BlockSpec((1, HEAD_W), lambda b, pt: (0, 0)),
                      pl.BlockSpec(memory_space=pl.ANY),
                      pl.BlockSpec(memory_space=pl.ANY)],
            out_specs=pl.BlockSpec((None, 1, ATTN_CH), lambda b, pt: (b, 0, 0)),
            scratch_shapes=[pltpu.VMEM((2, past, ATTN_CH), F32),
                            pltpu.VMEM((2, past, ATTN_CH), F32),
                            pltpu.SemaphoreType.DMA((2,)),
                            pltpu.SemaphoreType.DMA((2,))]),
        compiler_params=pltpu.CompilerParams(dimension_semantics=("arbitrary",),
                                             vmem_limit_bytes=VMEM_LIMIT),
        name="attn_s",
    )(page_table, qb, kn, vn, lam4, subln_g, cache_k, cache_v)


def _rope_tables(pos):
    half = ROT_DIM // 2
    inv_freq = ROPE_THETA ** (-jnp.arange(0, ROT_DIM, 2, dtype=F32) / ROT_DIM)
    ang = pos.astype(F32)[:, None] * inv_freq[None, :]
    cos, sin = jnp.cos(ang), jnp.sin(ang)
    n = pos.shape[0]
    ones = jnp.ones((n, HEAD_DIM - ROT_DIM), F32)
    zeros = jnp.zeros((n, HEAD_DIM - ROT_DIM), F32)
    zh = jnp.zeros((n, half), F32)
    c = jnp.concatenate([cos, cos, ones], axis=-1)
    sa = jnp.concatenate([-sin, zh, zeros], axis=-1)
    sb = jnp.concatenate([zh, sin, zeros], axis=-1)
    two = lambda a: jnp.concatenate([a, a], axis=-1)
    return two(c), two(sa), two(sb)


def kernel(x_prompt, x_sample, cache_k, cache_v, state_conv, page_table, c_prompt, c_sample, norm_mix_g, norm_ffn_g, norm_final_g, w_ada, b_ada, w_in, conv_dw_w, conv_dw_b, conv_ln_g, conv_ln_b, lambda_q1, lambda_k1, lambda_q2, lambda_k2, subln_g, w_out, w_ffn_in, w_ffn_out):
    B, S, _ = x_prompt.shape
    DB, T, _ = x_sample.shape
    depth = w_in.shape[0]
    assert depth == 1 and T == 1
    n_pages = page_table.shape[1]
    page = cache_k.shape[2]
    past = n_pages * page
    lam_init = _lambda_init(0)

    w_in_b = w_in[0].astype(BF16)
    wo_b = w_out[0].astype(BF16)
    wi_b = w_ffn_in[0].astype(BF16)
    w2_b = w_ffn_out[0].astype(BF16)
    g_mix = norm_mix_g[0][None, :]
    g_ffn = norm_ffn_g[0][None, :]
    g_fin = norm_final_g[None, :]
    dw_w = conv_dw_w[0]
    dw_b = conv_dw_b[0][None, :]
    ln_g = conv_ln_g[0][None, :]
    ln_b = conv_ln_b[0][None, :]
    lam4 = jnp.stack([lambda_q1[0], lambda_k1[0], lambda_q2[0], lambda_k2[0]])
    sg = subln_g[0][None, :]

    pad = (-B) % 8
    c_all = jnp.concatenate([c_prompt, jnp.zeros((pad, D_MODEL), F32), c_sample], axis=0)
    mods = _ada(c_all, w_ada[0], b_ada[0][None, :])
    mods_p = mods[:B].reshape(B, 1, 6 * D_MODEL)
    mods_s = mods[B + pad:]

    rc, ra, rb = _rope_tables(jnp.arange(S, dtype=jnp.int32))
    tm = 512
    kf, vf, qb, kb, vb, co, cst = _inproj_p(x_prompt, mods_p, mods_p, g_mix, w_in_b, dw_w, dw_b, ln_g, ln_b,
                                            rc, ra, rb, tm)
    ao = _attn_p(qb, kb, vb, lam4, sg, 512, lam_init)
    tmm = 512
    per_tile = S // tmm
    mod_specs_p = [pl.BlockSpec((None, 1, D_MODEL), (lambda i, k=k: (i // per_tile, 0, k))) for k in (2, 3, 4, 5)]
    y_p = _mlp(x_prompt.reshape(B * S, D_MODEL), co.reshape(B * S, CONV_CH), ao.reshape(B * S, ATTN_CH),
               mods_p, mod_specs_p, g_ffn, g_fin, wo_b, wi_b, w2_b, tmm, "mlp_p")

    rcs, ras, rbs = _rope_tables(jnp.full((1,), past, dtype=jnp.int32))
    xs = x_sample.reshape(DB, D_MODEL)
    u_s, kn, vn, qs, co_s = _inproj_s(xs, mods_s, g_mix, w_in_b, state_conv[0], dw_w[:CONV_WIDTH - 1],
                                      dw_w[CONV_WIDTH - 1:], dw_b, ln_g, ln_b, rcs, ras, rbs)
    ck = cache_k[0].reshape(-1, page, ATTN_CH)
    cv = cache_v[0].reshape(-1, page, ATTN_CH)
    ao_s = _attn_s(page_table, qs.reshape(DB, 1, ATTN_CH), kn.reshape(DB, 1, ATTN_CH), vn.reshape(DB, 1, ATTN_CH),
                   lam4, sg, ck, cv, lam_init)
    mod_specs_s = [pl.BlockSpec((DB, D_MODEL), (lambda i, k=k: (0, k))) for k in (2, 3, 4, 5)]
    y_s = _mlp(xs, co_s, ao_s.reshape(DB, ATTN_CH), mods_s, mod_specs_s, g_ffn, g_fin, wo_b, wi_b, w2_b, DB, "mlp_s")

    y_prompt = y_p.reshape(B, S, D_MODEL)
    y_sample = y_s.reshape(DB, 1, D_MODEL)
    k_prompt = kf.reshape(1, B, S, N_HEADS, HEAD_W)
    v_prompt = vf.reshape(1, B, S, N_HEADS, HEAD_W)
    conv_prompt = cst[:, HIST - (CONV_WIDTH - 1):][None]
    k_sample = kn.reshape(1, DB, 1, N_HEADS, HEAD_W)
    v_sample = vn.reshape(1, DB, 1, N_HEADS, HEAD_W)
    conv_sample = jnp.concatenate([state_conv[0][:, 1:], u_s[:, None, :]], axis=1)[None]
    return (y_prompt, y_sample, k_prompt, v_prompt, conv_prompt, k_sample, v_sample, conv_sample)
```
